```python
import jax, jax.numpy as jnp
from jax import lax
import numpy as np

D_MODEL = 1024
BATCH = 1
SEQ = 16384
DEPTH = 2

GRID_W = 64
CTX_LEN = 256
HGRN_HEADS = 4
HGRN_KEY = 128
HGRN_VAL = 128
HGRN_KDIM = HGRN_HEADS * HGRN_KEY
HGRN_WIDTH = HGRN_HEADS * HGRN_VAL
CONV_WIDTH = D_MODEL - HGRN_WIDTH
CONV_KERNEL = 31
CONV_PAD = CONV_KERNEL // 2
CHUNK = 64
D_FF = (((8 * D_MODEL + 2) // 3 + 255) // 256) * 256
N_MOD = 6
EPS = 1e-6
F_MIN = 1e-30
PROJ_SIZES = (HGRN_KDIM, HGRN_KDIM, HGRN_KDIM, HGRN_WIDTH, HGRN_WIDTH, CONV_WIDTH, CONV_WIDTH)
D_IN = sum(PROJ_SIZES)

kernel_name = "hybrid_hgrn2_conformer_dit_block"


def rms_norm(x, g):
    xf = x.astype(jnp.float32)
    y = xf * lax.rsqrt(jnp.mean(xf * xf, axis=-1, keepdims=True) + EPS)
    return (y * g.astype(jnp.float32)).astype(x.dtype)


def layer_norm(x, g, b):
    xf = x.astype(jnp.float32)
    mu = jnp.mean(xf, axis=-1, keepdims=True)
    var = jnp.mean(jnp.square(xf - mu), axis=-1, keepdims=True)
    y = (xf - mu) * lax.rsqrt(var + EPS)
    return (y * g.astype(jnp.float32) + b.astype(jnp.float32)).astype(x.dtype)


def modulate(x, gain, shift, scale):
    return rms_norm(x, gain) * (1 + scale) + shift


def split_proj(p):
    return jnp.split(p, list(np.cumsum(PROJ_SIZES)[:-1]), axis=-1)


def lower_bound(P, l):
    p = jax.nn.softmax(P.astype(jnp.float32), axis=0)
    return (jnp.cumsum(p, axis=0) - p[0])[l]


def forget_gate(pre, lb):
    s = jax.nn.sigmoid(pre.astype(jnp.float32))
    f = lb + (1 - lb) * s
    one_minus_f = (1 - lb) * (1 - s)
    return one_minus_f, jnp.log(jnp.maximum(f, F_MIN))


def hgrn_inputs(p, lb_f, lb_b):
    B, L = p[0].shape[:2]
    heads = lambda t, d: t.astype(jnp.float32).reshape(B, L, HGRN_HEADS, d)
    q = heads(p[0], HGRN_KEY) * (HGRN_KEY ** -0.5)
    k_f, lf_f = forget_gate(p[1], lb_f)
    k_b, lf_b = forget_gate(p[2], lb_b)
    v = heads(p[3], HGRN_VAL)
    return (q, heads(k_f, HGRN_KEY), heads(k_b, HGRN_KEY), v,
            heads(lf_f, HGRN_KEY), heads(lf_b, HGRN_KEY))


def gla_scan(q, k, v, logf, s0):
    B, L, H, _ = q.shape
    V = v.shape[-1]
    n = L // CHUNK
    to_chunks = lambda t: t.reshape(B, n, CHUNK, H, t.shape[-1]).transpose(1, 0, 3, 2, 4)
    mask = jnp.tril(jnp.ones((CHUNK, CHUNK), bool))[:, :, None]

    def step(S, inp):
        qc, kc, vc, gc = inp
        b = jnp.cumsum(gc, axis=2)
        o_inter = jnp.einsum('bhtk,bhkv->bhtv', qc * jnp.exp(b), S)
        diff = b[:, :, :, None, :] - b[:, :, None, :, :]
        decay = jnp.where(mask, jnp.exp(jnp.where(mask, diff, 0.0)), 0.0)
        A = jnp.einsum('bhtk,bhtsk,bhsk->bhts', qc, decay, kc)
        o = o_inter + jnp.einsum('bhts,bhsv->bhtv', A, vc)
        b_last = b[:, :, -1:, :]
        S_new = (jnp.exp(b_last[:, :, 0, :])[..., None] * S
                 + jnp.einsum('bhsk,bhsv->bhkv', kc * jnp.exp(b_last - b), vc))
        return S_new, o

    S, o = lax.scan(step, s0, (to_chunks(q), to_chunks(k), to_chunks(v), to_chunks(logf)))
    o = o.transpose(1, 0, 3, 2, 4).reshape(B, L, H, V)
    return o, S


def hgrn_bidir(q, k_f, k_b, v, lf_f, lf_b, s0_f, s0_b):
    flip = lambda t: jnp.flip(t, axis=1)
    o_f, s_f = gla_scan(q, k_f, v, lf_f, s0_f)
    o_b, s_b = gla_scan(flip(q), flip(k_b), flip(v), flip(lf_b), s0_b)
    return o_f + flip(o_b), s_f, s_b


def hgrn_readout(o, p_g, onorm_g):
    B, L = o.shape[:2]
    y = o * lax.rsqrt(jnp.mean(o * o, axis=-1, keepdims=True) + EPS) * onorm_g.astype(jnp.float32)
    y = y.reshape(B, L, HGRN_WIDTH) * jax.nn.silu(p_g.astype(jnp.float32))
    return y.astype(p_g.dtype)


def conv_branch(p_a, p_b, dw_w, dw_b, ln_g, ln_b, rows):
    u = p_a * jax.nn.sigmoid(p_b)
    B, L, C = u.shape
    seqs = u.reshape(B * rows, L // rows, C)
    y = lax.conv_general_dilated(seqs, dw_w.reshape(CONV_KERNEL, 1, C).astype(u.dtype),
                                 window_strides=(1,), padding=[(CONV_PAD, CONV_PAD)],
                                 dimension_numbers=('NWC', 'WIO', 'NWC'),
                                 feature_group_count=C)
    y = y.reshape(B, L, C) + dw_b
    return jax.nn.silu(layer_norm(y, ln_g, ln_b))


def mixer_output(o, p, rows, w_out, onorm_g, dw_w, dw_b, ln_g, ln_b):
    a = hgrn_readout(o, p[4], onorm_g)
    b = conv_branch(p[5], p[6], dw_w, dw_b, ln_g, ln_b, rows)
    return jnp.concatenate([a, b], axis=-1) @ w_out


def swiglu(h, wg, wu, wd):
    return (jax.nn.silu(h @ wg) * (h @ wu)) @ wd


def setup_inputs(seed: int = 0) -> dict:
    key = jax.random.key(seed)
    ks = jax.random.split(key, 24)
    f32 = jnp.float32
    nrm = lambda k, shape, s: jax.random.normal(k, shape, f32) * s
    gain = lambda k, shape: 1.0 + 0.05 * jax.random.normal(k, shape, f32)
    return {
        "x": nrm(ks[0], (BATCH, SEQ, D_MODEL), 1.0),
        "c": nrm(ks[1], (BATCH, D_MODEL), 1.0),
        "ctx": nrm(ks[2], (BATCH, CTX_LEN, D_MODEL), 1.0),
        "c_ctx": nrm(ks[3], (D_MODEL,), 1.0),
        "w_mod": nrm(ks[4], (DEPTH, D_MODEL, N_MOD * D_MODEL), 0.5 * D_MODEL ** -0.5),
        "b_mod": nrm(ks[5], (DEPTH, N_MOD * D_MODEL), 0.02),
        "pre_mix_g": gain(ks[6], (DEPTH, D_MODEL)),
        "post_mix_g": gain(ks[7], (DEPTH, D_MODEL)),
        "pre_ffn_g": gain(ks[8], (DEPTH, D_MODEL)),
        "post_ffn_g": gain(ks[9], (DEPTH, D_MODEL)),
        "w_in": nrm(ks[10], (DEPTH, D_MODEL, D_IN), D_MODEL ** -0.5),
        "w_out": nrm(ks[11], (DEPTH, D_MODEL, D_MODEL), D_MODEL ** -0.5),
        "hgrn_lb_fwd": nrm(ks[12], (DEPTH, HGRN_KDIM), 1.0),
        "hgrn_lb_bwd": nrm(ks[13], (DEPTH, HGRN_KDIM), 1.0),
        "hgrn_onorm_g": gain(ks[14], (DEPTH, HGRN_VAL)),
        "conv_dw_w": nrm(ks[15], (DEPTH, CONV_KERNEL, CONV_WIDTH), CONV_KERNEL ** -0.5),
        "conv_dw_b": nrm(ks[16], (DEPTH, CONV_WIDTH), 0.02),
        "conv_ln_g": gain(ks[17], (DEPTH, CONV_WIDTH)),
        "conv_ln_b": nrm(ks[18], (DEPTH, CONV_WIDTH), 0.02),
        "ffn_w_gate": nrm(ks[19], (DEPTH, D_MODEL, D_FF), D_MODEL ** -0.5),
        "ffn_w_up": nrm(ks[20], (DEPTH, D_MODEL, D_FF), D_MODEL ** -0.5),
        "ffn_w_down": nrm(ks[21], (DEPTH, D_FF, D_MODEL), D_FF ** -0.5),
    }


def reference(x, c, ctx, c_ctx, w_mod, b_mod, pre_mix_g, post_mix_g, pre_ffn_g, post_ffn_g,
              w_in, w_out, hgrn_lb_fwd, hgrn_lb_bwd, hgrn_onorm_g, conv_dw_w, conv_dw_b,
              conv_ln_g, conv_ln_b, ffn_w_gate, ffn_w_up, ffn_w_down):
    B, L, _ = x.shape
    rows = L // GRID_W
    zero_state = jnp.zeros((B, HGRN_HEADS, HGRN_KEY, HGRN_VAL), jnp.float32)
    sc = jax.nn.silu(c)
    scc = jax.nn.silu(c_ctx)[None]
    xc = ctx
    for l in range(DEPTH):
        last = l == DEPTH - 1
        mx = jnp.split((sc @ w_mod[l] + b_mod[l])[:, None, :], N_MOD, axis=-1)
        mc = jnp.split((scc @ w_mod[l] + b_mod[l])[:, None, :], N_MOD, axis=-1)
        lb_f = lower_bound(hgrn_lb_fwd, l)
        lb_b = lower_bound(hgrn_lb_bwd, l)

        px = split_proj(modulate(x, pre_mix_g[l], mx[0], mx[1]) @ w_in[l])
        pc = split_proj(modulate(xc, pre_mix_g[l], mc[0], mc[1]) @ w_in[l])
        oc, s_f, s_b = hgrn_bidir(*hgrn_inputs(pc, lb_f, lb_b), zero_state, zero_state)
        ox, _, _ = hgrn_bidir(*hgrn_inputs(px, lb_f, lb_b), s_f, s_b)
        mix_x = mixer_output(ox, px, rows, w_out[l], hgrn_onorm_g[l], conv_dw_w[l],
                             conv_dw_b[l], conv_ln_g[l], conv_ln_b[l])
        x = x + mx[2] * rms_norm(mix_x, post_mix_g[l])

        hx = swiglu(modulate(x, pre_ffn_g[l], mx[3], mx[4]), ffn_w_gate[l], ffn_w_up[l], ffn_w_down[l])
        x = x + mx[5] * rms_norm(hx, post_ffn_g[l])

        if not last:
            mix_c = mixer_output(oc, pc, 1, w_out[l], hgrn_onorm_g[l], conv_dw_w[l],
                                 conv_dw_b[l], conv_ln_g[l], conv_ln_b[l])
            xc = xc + mc[2] * rms_norm(mix_c, post_mix_g[l])
            hc = swiglu(modulate(xc, pre_ffn_g[l], mc[3], mc[4]), ffn_w_gate[l], ffn_w_up[l], ffn_w_down[l])
            xc = xc + mc[5] * rms_norm(hc, post_ffn_g[l])
    return x
```

```python
import functools

import jax
import jax.numpy as jnp
from jax import lax
from jax.experimental import pallas as pl
from jax.experimental.pallas import tpu as pltpu

F32 = jnp.float32
BF16 = jnp.bfloat16

D_MODEL = 1024
DEPTH = 2
GRID_W = 64
HEADS = 4
HEAD_DIM = 128
HG = HEADS * HEAD_DIM
CONV_C = D_MODEL - HG
CONV_K = 31
CONV_PAD = CONV_K // 2
CONV_HALO = 16
D_FF = 2816
N_MOD = 6
EPS = 1e-6
F_MIN = 1e-30
N_PROJ = 7

CHUNK = 64
HALF = CHUNK // 2
SAFE_DECAY = 60.0

FF_SLICE = 256
VMEM_LIMIT = 56 * 1024 * 1024


def _sigmoid(x):
    return 1.0 / (1.0 + jnp.exp(-x))


def _const_spec(shape):
    return pl.BlockSpec(shape, lambda *_: (0,) * len(shape), pipeline_mode=pl.Buffered(1))


def _mod_kernel(c_ref, w_ref, b_ref, o_ref):
    s = c_ref[...]
    s = s * _sigmoid(s)
    o_ref[0] = jnp.dot(s, w_ref[0], precision=lax.Precision.HIGHEST,
                       preferred_element_type=F32) + b_ref[0]


def _modulation(c2, w_mod, b_mod):
    tn = 1536
    n = N_MOD * D_MODEL
    return pl.pallas_call(
        _mod_kernel,
        out_shape=jax.ShapeDtypeStruct((DEPTH, 8, n), F32),
        grid=(DEPTH, n // tn),
        in_specs=[pl.BlockSpec((8, D_MODEL), lambda l, j: (0, 0)),
                  pl.BlockSpec((1, D_MODEL, tn), lambda l, j: (l, 0, j)),
                  pl.BlockSpec((1, 1, tn), lambda l, j: (l, 0, j))],
        out_specs=pl.BlockSpec((1, 8, tn), lambda l, j: (l, 0, j)),
        compiler_params=pltpu.CompilerParams(
            dimension_semantics=("arbitrary", "arbitrary"), vmem_limit_bytes=VMEM_LIMIT),
        name="modulation",
    )(c2, w_mod, b_mod.reshape(DEPTH, 1, n))


def _lower_bound(p_ref, layer):
    p = p_ref[...]
    e = jnp.exp(p - jnp.max(p, axis=0, keepdims=True))
    sm = e / jnp.sum(e, axis=0, keepdims=True)
    lb = jnp.zeros((1, HG), F32)
    for i in range(1, layer + 1):
        lb = lb + sm[i:i + 1, :]
    return lb


def _forget(pre, lb):
    s = _sigmoid(pre)
    f = lb + (1.0 - lb) * s
    k = (1.0 - lb) * (1.0 - s)
    return k, jnp.log(jnp.maximum(f, F_MIN))


def _premix_kernel(layer, x_ref, m_ref, g_ref, w_ref, lbf_ref, lbb_ref,
                   q_ref, kf_ref, lff_ref, kb_ref, lfb_ref, v_ref, gs_ref, u_ref):
    x = x_ref[...]
    y = x * lax.rsqrt(jnp.mean(x * x, axis=-1, keepdims=True) + EPS) * g_ref[...]
    h = (y * (1.0 + m_ref[1:2, :]) + m_ref[0:1, :]).astype(BF16)

    def proj(j):
        return jnp.dot(h, w_ref[:, j * HG:(j + 1) * HG], preferred_element_type=F32)

    q_ref[...] = (proj(0) * (HEAD_DIM ** -0.5)).astype(BF16)
    k, lf = _forget(proj(1), _lower_bound(lbf_ref, layer))
    kf_ref[...] = k.astype(BF16)
    lff_ref[...] = lf
    k, lf = _forget(proj(2), _lower_bound(lbb_ref, layer))
    kb_ref[...] = k.astype(BF16)
    lfb_ref[...] = lf
    v_ref[...] = proj(3).astype(BF16)
    g = proj(4)
    gs_ref[...] = (g * _sigmoid(g)).astype(BF16)
    u_ref[...] = (proj(5) * _sigmoid(proj(6))).astype(BF16)


def _premix(layer, x, mod6, gain, w_in, lbf, lbb):
    n = x.shape[0]
    tm = min(512, n)
    row = lambda i: (i, 0)
    bf = jax.ShapeDtypeStruct((n, HG), BF16)
    f32 = jax.ShapeDtypeStruct((n, HG), F32)
    return pl.pallas_call(
        functools.partial(_premix_kernel, layer),
        out_shape=(bf, bf, f32, bf, f32, bf, bf, bf),
        grid=(n // tm,),
        in_specs=[pl.BlockSpec((tm, D_MODEL), row),
                  _const_spec((N_MOD, D_MODEL)),
                  _const_spec((1, D_MODEL)),
                  _const_spec((D_MODEL, N_PROJ * HG)),
                  _const_spec((DEPTH, HG)),
                  _const_spec((DEPTH, HG))],
        out_specs=tuple(pl.BlockSpec((tm, HG), row) for _ in range(8)),
        compiler_params=pltpu.CompilerParams(
            dimension_semantics=("parallel",), vmem_limit_bytes=VMEM_LIMIT),
        name="premix",
    )(x, mod6, gain, w_in, lbf, lbb)


def _split3(g):
    g1 = g.astype(BF16)
    r = g - g1.astype(F32)
    g2 = r.astype(BF16)
    g3 = (r - g2.astype(F32)).astype(BF16)
    return g1, g2, g3


def _cum_decay(g_ref, rows, rev):
    ri = lax.broadcasted_iota(jnp.int32, (CHUNK, CHUNK), 0)
    ci = lax.broadcasted_iota(jnp.int32, (CHUNK, CHUNK), 1)
    tri = jnp.where((ci >= ri) if rev else (ci <= ri), 1.0, 0.0).astype(BF16)
    b = None
    for part in _split3(g_ref[rows, :]):
        t = jnp.dot(tri, part, preferred_element_type=F32)
        b = t if b is None else b + t
    return b


def _chunk_scores(q_ref, k_ref, b_sc, a_sc, rows, rev):
    mid = HALF if rev else HALF - 1
    for h in range(HEADS):
        hl = slice(h * HEAD_DIM, (h + 1) * HEAD_DIM)
        b = b_sc[:, hl]
        r = b[mid:mid + 1, :]
        q1 = (q_ref[rows, hl].astype(F32) * jnp.exp(b - r)).astype(BF16)
        k1 = (k_ref[rows, hl].astype(F32) * jnp.exp(r - b)).astype(BF16)
        a_sc[h] = lax.dot_general(q1, k1, (((1,), (1,)), ((), ())), preferred_element_type=F32)


def _chunk_scores_exact(q_ref, k_ref, b_sc, a_sc, rows):
    lane = lax.broadcasted_iota(jnp.int32, (CHUNK, CHUNK), 1)
    row = lax.broadcasted_iota(jnp.int32, (CHUNK, HEAD_DIM), 0)
    for h in range(HEADS):
        hl = slice(h * HEAD_DIM, (h + 1) * HEAD_DIM)
        q = q_ref[rows, hl].astype(F32)
        k = k_ref[rows, hl].astype(F32)
        b = b_sc[:, hl]

        def body(s, a):
            ks = jnp.sum(jnp.where(row == s, k, 0.0), axis=0, keepdims=True)
            bs = jnp.sum(jnp.where(row == s, b, 0.0), axis=0, keepdims=True)
            col = jnp.sum(q * jnp.exp(jnp.minimum(b - bs, 0.0)) * ks, axis=-1, keepdims=True)
            return jnp.where(lane == s, col, a)

        a_sc[h] = lax.fori_loop(0, CHUNK, body, jnp.zeros((CHUNK, CHUNK), F32))


def _chunk_output(q_ref, k_ref, v_ref, o_ref, s_ref, b_sc, a_sc, rows, rev):
    last = 0 if rev else CHUNK - 1
    ri = lax.broadcasted_iota(jnp.int32, (CHUNK, CHUNK), 0)
    ci = lax.broadcasted_iota(jnp.int32, (CHUNK, CHUNK), 1)
    keep = (ci >= ri) if rev else (ci <= ri)
    for h in range(HEADS):
        hl = slice(h * HEAD_DIM, (h + 1) * HEAD_DIM)
        b = b_sc[:, hl]
        bl = b[last:last + 1, :]
        v = v_ref[rows, hl]
        st = s_ref[h]
        q2 = (q_ref[rows, hl].astype(F32) * jnp.exp(b)).astype(BF16)
        a = jnp.where(keep, a_sc[h], 0.0).astype(BF16)
        o = lax.dot_general(q2, st.astype(BF16), (((1,), (1,)), ((), ())), preferred_element_type=F32)
        o = o + jnp.dot(a, v, preferred_element_type=F32)
        o_ref[rows, hl] = o.astype(o_ref.dtype)
        k2 = (k_ref[rows, hl].astype(F32) * jnp.exp(bl - b)).astype(BF16)
        s_ref[h] = st * jnp.exp(bl) + lax.dot_general(
            v, k2, (((0,), (0,)), ((), ())), preferred_element_type=F32)


def _scan_kernel(n_chunks, qf_ref, kf_ref, gf_ref, vf_ref, qb_ref, kb_ref, gb_ref, vb_ref,
                 s0f_ref, s0b_ref, of_ref, ob_ref, sf_ref, sb_ref,
                 bf_sc, bb_sc, af_sc, ab_sc):
    @pl.when(pl.program_id(0) == 0)
    def _():
        sf_ref[...] = s0f_ref[...]
        sb_ref[...] = s0b_ref[...]

    def body(c, carry):
        rows_f = pl.ds(pl.multiple_of(c * CHUNK, CHUNK), CHUNK)
        rows_b = pl.ds(pl.multiple_of((n_chunks - 1 - c) * CHUNK, CHUNK), CHUNK)
        bf = _cum_decay(gf_ref, rows_f, False)
        bb = _cum_decay(gb_ref, rows_b, True)
        bf_sc[...] = bf
        bb_sc[...] = bb
        rf = bf[HALF - 1:HALF, :]
        rb = bb[HALF:HALF + 1, :]
        worst = jnp.maximum(jnp.maximum(-rf, rf - bf[CHUNK - 1:CHUNK, :]),
                            jnp.maximum(-rb, rb - bb[0:1, :]))
        unsafe = jnp.max(worst) > SAFE_DECAY

        _chunk_scores(qf_ref, kf_ref, bf_sc, af_sc, rows_f, False)
        _chunk_scores(qb_ref, kb_ref, bb_sc, ab_sc, rows_b, True)

        @pl.when(unsafe)
        def _():
            _chunk_scores_exact(qf_ref, kf_ref, bf_sc, af_sc, rows_f)
            _chunk_scores_exact(qb_ref, kb_ref, bb_sc, ab_sc, rows_b)

        _chunk_output(qf_ref, kf_ref, vf_ref, of_ref, sf_ref, bf_sc, af_sc, rows_f, False)
        _chunk_output(qb_ref, kb_ref, vb_ref, ob_ref, sb_ref, bb_sc, ab_sc, rows_b, True)
        return carry

    lax.fori_loop(0, n_chunks, body, 0)


def _scan(q, kf, lff, kb, lfb, v, s0f, s0b):
    n = q.shape[0]
    tb = min(512, n)
    nb = n // tb
    fwd = lambda i: (i, 0)
    bwd = lambda i: (nb - 1 - i, 0)
    blk = lambda im: pl.BlockSpec((tb, HG), im)
    st_shape = (HEADS, HEAD_DIM, HEAD_DIM)
    st_spec = pl.BlockSpec(st_shape, lambda i: (0, 0, 0))
    return pl.pallas_call(
        functools.partial(_scan_kernel, tb // CHUNK),
        out_shape=(jax.ShapeDtypeStruct((n, HG), BF16), jax.ShapeDtypeStruct((n, HG), BF16),
                   jax.ShapeDtypeStruct(st_shape, F32), jax.ShapeDtypeStruct(st_shape, F32)),
        grid=(nb,),
        in_specs=[blk(fwd), blk(fwd), blk(fwd), blk(fwd),
                  blk(bwd), blk(bwd), blk(bwd), blk(bwd), st_spec, st_spec],
        out_specs=(blk(fwd), blk(bwd), st_spec, st_spec),
        scratch_shapes=[pltpu.VMEM((CHUNK, HG), F32), pltpu.VMEM((CHUNK, HG), F32),
                        pltpu.VMEM((HEADS, CHUNK, CHUNK), F32), pltpu.VMEM((HEADS, CHUNK, CHUNK), F32)],
        compiler_params=pltpu.CompilerParams(
            dimension_semantics=("arbitrary",), vmem_limit_bytes=VMEM_LIMIT),
        name="scan",
    )(q, kf, lff, v, q, kb, lfb, v, s0f, s0b)


def _rms(x, g):
    return x * lax.rsqrt(jnp.mean(x * x, axis=-1, keepdims=True) + EPS) * g


def _mixffn_kernel(seg, x_ref, of_ref, ob_ref, gs_ref, u_ref, m_ref, on_ref, dww_ref, dwb_ref,
                   lng_ref, lnb_ref, wo_ref, pmg_ref, pfg_ref, pog_ref, wg_ref, wu_ref, wd_ref,
                   out_ref, pad_sc):
    tm = x_ref.shape[0]
    n_seg = tm // seg

    o = of_ref[...].astype(F32) + ob_ref[...].astype(F32)
    heads = []
    for h in range(HEADS):
        oh = o[:, h * HEAD_DIM:(h + 1) * HEAD_DIM]
        heads.append(oh * lax.rsqrt(jnp.mean(oh * oh, axis=-1, keepdims=True) + EPS) * on_ref[...])
    a = (jnp.concatenate(heads, axis=-1) * gs_ref[...].astype(F32)).astype(BF16)

    zeros = jnp.zeros((n_seg, CONV_HALO, CONV_C), F32)
    pad_sc[:, 0:CONV_HALO, :] = zeros
    pad_sc[:, CONV_HALO + seg:2 * CONV_HALO + seg, :] = zeros
    pad_sc[:, CONV_HALO:CONV_HALO + seg, :] = u_ref[...].astype(F32).reshape(n_seg, seg, CONV_C)
    y = None
    for j in range(CONV_K):
        off = CONV_HALO - CONV_PAD + j
        t = pad_sc[:, off:off + seg, :] * dww_ref[j:j + 1, :]
        y = t if y is None else y + t
    y = y.reshape(tm, CONV_C) + dwb_ref[...]
    mu = jnp.mean(y, axis=-1, keepdims=True)
    yc = y - mu
    var = jnp.mean(yc * yc, axis=-1, keepdims=True)
    z = yc * lax.rsqrt(var + EPS) * lng_ref[...] + lnb_ref[...]
    b = (z * _sigmoid(z)).astype(BF16)

    mix = jnp.dot(jnp.concatenate([a, b], axis=-1), wo_ref[...], preferred_element_type=F32)
    x1 = x_ref[...] + m_ref[2:3, :] * _rms(mix, pmg_ref[...])

    h = (_rms(x1, pfg_ref[...]) * (1.0 + m_ref[4:5, :]) + m_ref[3:4, :]).astype(BF16)
    acc = None
    for s in range(D_FF // FF_SLICE):
        cols = slice(s * FF_SLICE, (s + 1) * FF_SLICE)
        gate = jnp.dot(h, wg_ref[:, cols], preferred_element_type=F32)
        up = jnp.dot(h, wu_ref[:, cols], preferred_element_type=F32)
        t = jnp.dot((gate * _sigmoid(gate) * up).astype(BF16), wd_ref[cols, :],
                    preferred_element_type=F32)
        acc = t if acc is None else acc + t
    out_ref[...] = x1 + m_ref[5:6, :] * _rms(acc, pog_ref[...])


def _mixffn(seg, x, o_f, o_b, gs, u, mod6, onorm, dw_w, dw_b, ln_g, ln_b, w_out,
            post_mix_g, pre_ffn_g, post_ffn_g, wg, wu, wd):
    n = x.shape[0]
    tm = min(256, n)
    row = lambda i: (i, 0)
    half = pl.BlockSpec((tm, HG), row)
    vec = lambda c: _const_spec((1, c))
    return pl.pallas_call(
        functools.partial(_mixffn_kernel, seg),
        out_shape=jax.ShapeDtypeStruct((n, D_MODEL), F32),
        grid=(n // tm,),
        in_specs=[pl.BlockSpec((tm, D_MODEL), row), half, half, half, half,
                  _const_spec((N_MOD, D_MODEL)), vec(HEAD_DIM),
                  _const_spec((CONV_K, CONV_C)), vec(CONV_C), vec(CONV_C), vec(CONV_C),
                  _const_spec((D_MODEL, D_MODEL)), vec(D_MODEL), vec(D_MODEL), vec(D_MODEL),
                  _const_spec((D_MODEL, D_FF)), _const_spec((D_MODEL, D_FF)),
                  _const_spec((D_FF, D_MODEL))],
        out_specs=pl.BlockSpec((tm, D_MODEL), row),
        scratch_shapes=[pltpu.VMEM((tm // seg, seg + 2 * CONV_HALO, CONV_C), F32)],
        compiler_params=pltpu.CompilerParams(
            dimension_semantics=("parallel",), vmem_limit_bytes=VMEM_LIMIT),
        name="mixffn",
    )(x, o_f, o_b, gs, u, mod6, onorm, dw_w, dw_b, ln_g, ln_b, w_out,
      post_mix_g, pre_ffn_g, post_ffn_g, wg, wu, wd)


def kernel(x, c, ctx, c_ctx, w_mod, b_mod, pre_mix_g, post_mix_g, pre_ffn_g, post_ffn_g, w_in, w_out, hgrn_lb_fwd, hgrn_lb_bwd, hgrn_onorm_g, conv_dw_w, conv_dw_b, conv_ln_g, conv_ln_b, ffn_w_gate, ffn_w_up, ffn_w_down):
    batch, seq, _ = x.shape
    assert batch == 1 and seq % GRID_W == 0 and GRID_W == CHUNK
    xs = x[0]
    xc = ctx[0]
    c2 = jnp.zeros((8, D_MODEL), F32).at[0].set(c[0]).at[1].set(c_ctx)
    mods = _modulation(c2, w_mod, b_mod)
    zero_state = jnp.zeros((HEADS, HEAD_DIM, HEAD_DIM), F32)

    for l in range(DEPTH):
        mx = mods[l, 0].reshape(N_MOD, D_MODEL)
        mc = mods[l, 1].reshape(N_MOD, D_MODEL)
        w_in_l = w_in[l].astype(BF16)
        vec = lambda p: p[l][None, :]
        mix_args = (vec(hgrn_onorm_g), conv_dw_w[l], vec(conv_dw_b), vec(conv_ln_g), vec(conv_ln_b),
                    w_out[l].astype(BF16), vec(post_mix_g), vec(pre_ffn_g), vec(post_ffn_g),
                    ffn_w_gate[l].astype(BF16), ffn_w_up[l].astype(BF16), ffn_w_down[l].astype(BF16))

        qc, kfc, lffc, kbc, lfbc, vc, gsc, uc = _premix(
            l, xc, mc, vec(pre_mix_g), w_in_l, hgrn_lb_fwd, hgrn_lb_bwd)
        ocf, ocb, s_f, s_b = _scan(qc, kfc, lffc, kbc, lfbc, vc, zero_state, zero_state)
        qx, kfx, lffx, kbx, lfbx, vx, gsx, ux = _premix(
            l, xs, mx, vec(pre_mix_g), w_in_l, hgrn_lb_fwd, hgrn_lb_bwd)
        oxf, oxb, _, _ = _scan(qx, kfx, lffx, kbx, lfbx, vx, s_f, s_b)
        xs = _mixffn(GRID_W, xs, oxf, oxb, gsx, ux, mx, *mix_args)
        if l != DEPTH - 1:
            xc = _mixffn(xc.shape[0], xc, ocf, ocb, gsc, uc, mc, *mix_args)
    return xs[None]
```
